```python
import math
import jax, jax.numpy as jnp
from jax import lax
import numpy as np

D_MODEL = 1024
BATCH = 8
SEQ = 2048
DEPTH = 4

SSD_D_INNER = D_MODEL
SSD_HEADDIM = 64
SSD_HEADS = SSD_D_INNER // SSD_HEADDIM
SSD_GROUPS = 2
SSD_HPG = SSD_HEADS // SSD_GROUPS
SSD_STATE = 128
SSD_CONV = 4
SSD_CHUNK = 128
SSD_CONV_DIM = SSD_D_INNER + 2 * SSD_GROUPS * SSD_STATE
RET_HEADS = 4
RET_QK_DIM = 128
RET_HEAD_V = D_MODEL // RET_HEADS
RET_CHUNK = 128
ROPE_BASE = 10000.0
ATT_Q_HEADS = 16
ATT_KV_HEADS = 2
ATT_GROUP = ATT_Q_HEADS // ATT_KV_HEADS
ATT_HEAD_DIM = D_MODEL // ATT_Q_HEADS
WINDOW = 128
ATT_BLOCK = 128
REL_BUCKETS = 32
REL_MAX_EXACT = REL_BUCKETS // 2
REL_MAX_DIST = 128
N_BRANCH = 3
BRANCH_WIDTH = D_MODEL
D_FF = 2816
EPS = 1e-6
NEG = -1e30

IN_SPLIT_SIZES = (
    SSD_D_INNER,
    SSD_CONV_DIM,
    SSD_HEADS,
    RET_HEADS * RET_QK_DIM,
    RET_HEADS * RET_QK_DIM,
    RET_HEADS * RET_HEAD_V,
    RET_HEADS * RET_HEAD_V,
    ATT_Q_HEADS * ATT_HEAD_DIM,
    ATT_KV_HEADS * ATT_HEAD_DIM,
    ATT_KV_HEADS * ATT_HEAD_DIM,
    N_BRANCH * D_MODEL,
)
IN_WIDTH = sum(IN_SPLIT_SIZES)

kernel_name = "hybrid_ssd_retention_swa_macaron"


def rmsnorm(x, g):
    xf = x.astype(jnp.float32)
    y = xf * lax.rsqrt(jnp.mean(xf * xf, axis=-1, keepdims=True) + EPS)
    return (y * g.astype(jnp.float32)).astype(x.dtype)


def swiglu(x, w_in, w_out):
    a, b = jnp.split(x @ w_in, 2, axis=-1)
    return (jax.nn.silu(a) * b) @ w_out


def causal_dwconv(x, w, b):
    c = x.shape[-1]
    y = lax.conv_general_dilated(x, w[:, None, :].astype(x.dtype), (1,), [(SSD_CONV - 1, 0)],
                                 dimension_numbers=('NWC', 'WIO', 'NWC'), feature_group_count=c)
    return y + b


def ssd_mixer(z, xbc, dt_raw, conv_w, conv_b, dt_bias, a_log, d_skip, norm_g):
    bsz, L, _ = z.shape
    nc = L // SSD_CHUNK
    xbc = jax.nn.silu(causal_dwconv(xbc, conv_w, conv_b))
    xs, bm, cm = jnp.split(xbc, [SSD_D_INNER, SSD_D_INNER + SSD_GROUPS * SSD_STATE], axis=-1)
    xs = xs.reshape(bsz, nc, SSD_CHUNK, SSD_GROUPS, SSD_HPG, SSD_HEADDIM)
    bm = bm.reshape(bsz, nc, SSD_CHUNK, SSD_GROUPS, SSD_STATE)
    cm = cm.reshape(bsz, nc, SSD_CHUNK, SSD_GROUPS, SSD_STATE)
    dt = jax.nn.softplus(dt_raw.astype(jnp.float32) + dt_bias.astype(jnp.float32))
    dt = dt.reshape(bsz, nc, SSD_CHUNK, SSD_GROUPS, SSD_HPG)
    a = -jnp.exp(a_log.astype(jnp.float32)).reshape(SSD_GROUPS, SSD_HPG)
    da = jnp.moveaxis(dt * a, 2, -1)
    a_cs = jnp.cumsum(da, axis=-1)
    causal = jnp.tril(jnp.ones((SSD_CHUNK, SSD_CHUNK), dtype=bool))
    seg = a_cs[..., :, None] - a_cs[..., None, :]
    lmat = jnp.exp(jnp.where(causal, seg, -jnp.inf))
    xdt = xs * dt[..., None]
    cb = jnp.einsum('bclgn,bcsgn->bcgls', cm, bm)
    y_diag = jnp.einsum('bcgjls,bcsgjp->bclgjp', cb[:, :, :, None] * lmat, xdt)
    decay_st = jnp.moveaxis(jnp.exp(a_cs[..., -1:] - a_cs), -1, 2)
    states = jnp.einsum('bclgn,bclgjp->bcgjpn', bm, xdt * decay_st[..., None])
    chunk_decay = jnp.exp(a_cs[..., -1])

    def step(h, inp):
        s, dcy = inp
        return (h * dcy[..., None, None] + s).astype(h.dtype), h

    h0 = jnp.zeros_like(states[:, 0])
    _, prev = lax.scan(step, h0, (jnp.moveaxis(states, 1, 0), jnp.moveaxis(chunk_decay, 1, 0)))
    prev = jnp.moveaxis(prev, 0, 1)
    decay_out = jnp.moveaxis(jnp.exp(a_cs), -1, 2)
    y_off = jnp.einsum('bclgn,bcgjpn->bclgjp', cm, prev) * decay_out[..., None]
    y = y_diag + y_off + xs * d_skip.reshape(SSD_GROUPS, SSD_HPG, 1)
    y = y.reshape(bsz, L, SSD_D_INNER).astype(z.dtype)
    return rmsnorm(y * jax.nn.silu(z), norm_g)


def rope(x, cos, sin):
    x1, x2 = jnp.split(x, 2, axis=-1)
    c = cos[:, None, :]
    s = sin[:, None, :]
    return jnp.concatenate([x1 * c - x2 * s, x2 * c + x1 * s], axis=-1)


def retention_mixer(q, k, v, g, gn_g, cos, sin):
    bsz, L, _ = q.shape
    nc = L // RET_CHUNK
    q = rope(q.reshape(bsz, L, RET_HEADS, RET_QK_DIM), cos, sin)
    k = rope(k.reshape(bsz, L, RET_HEADS, RET_QK_DIM), cos, sin) * (RET_QK_DIM ** -0.5)
    v = v.reshape(bsz, L, RET_HEADS, RET_HEAD_V)
    log_g = jnp.log(1.0 - jnp.exp2(-5.0 - jnp.arange(RET_HEADS, dtype=jnp.float32)))
    idx = jnp.arange(RET_CHUNK, dtype=jnp.float32)
    diff = idx[:, None] - idx[None, :]
    dmat = jnp.where(diff >= 0, jnp.exp(jnp.maximum(diff, 0.0)[None] * log_g[:, None, None]), 0.0)
    qc = q.reshape(bsz, nc, RET_CHUNK, RET_HEADS, RET_QK_DIM)
    kc = k.reshape(bsz, nc, RET_CHUNK, RET_HEADS, RET_QK_DIM)
    vc = v.reshape(bsz, nc, RET_CHUNK, RET_HEADS, RET_HEAD_V)
    inner = jnp.einsum('bnhij,bnjhe->bnihe', jnp.einsum('bnihd,bnjhd->bnhij', qc, kc) * dmat, vc)
    zeta = jnp.exp((RET_CHUNK - 1.0 - idx)[:, None] * log_g[None, :])
    chunk_kv = jnp.einsum('bnjhd,bnjhe->bnhde', kc * zeta[..., None], vc)
    chunk_decay = jnp.exp(RET_CHUNK * log_g)

    def step(r, s):
        return (r * chunk_decay[:, None, None] + s).astype(r.dtype), r

    r0 = jnp.zeros_like(chunk_kv[:, 0])
    _, prev = lax.scan(step, r0, jnp.moveaxis(chunk_kv, 1, 0))
    prev = jnp.moveaxis(prev, 0, 1)
    xi = jnp.exp((idx + 1.0)[:, None] * log_g[None, :])
    cross = jnp.einsum('bnihd,bnhde->bnihe', qc, prev) * xi[..., None]
    o = (inner + cross).reshape(bsz, L, RET_HEADS, RET_HEAD_V).astype(jnp.float32)
    mu = jnp.mean(o, axis=-1, keepdims=True)
    var = jnp.mean(jnp.square(o - mu), axis=-1, keepdims=True)
    o = ((o - mu) * lax.rsqrt(var + EPS)).reshape(bsz, L, RET_HEADS * RET_HEAD_V)
    o = (o * gn_g.astype(jnp.float32)).astype(g.dtype)
    return o * jax.nn.silu(g)


def t5_bucket(dist):
    is_small = dist < REL_MAX_EXACT
    d = jnp.maximum(dist, 1).astype(jnp.float32)
    large = REL_MAX_EXACT + (jnp.log(d / REL_MAX_EXACT) / math.log(REL_MAX_DIST / REL_MAX_EXACT)
                             * (REL_BUCKETS - REL_MAX_EXACT)).astype(jnp.int32)
    large = jnp.minimum(large, REL_BUCKETS - 1)
    return jnp.where(is_small, dist, large)


def swa_sink_attention(q, k, v, sinks, band_bias, band_mask):
    bsz, L, _ = q.shape
    nb = L // ATT_BLOCK
    q = q.reshape(bsz, nb, ATT_BLOCK, ATT_KV_HEADS, ATT_GROUP, ATT_HEAD_DIM)
    k = k.reshape(bsz, L, ATT_KV_HEADS, ATT_HEAD_DIM)
    v = v.reshape(bsz, L, ATT_KV_HEADS, ATT_HEAD_DIM)

    def band(t):
        tp = jnp.pad(t, ((0, 0), (ATT_BLOCK, 0), (0, 0), (0, 0)))
        tp = tp.reshape(bsz, nb + 1, ATT_BLOCK, ATT_KV_HEADS, ATT_HEAD_DIM)
        return jnp.concatenate([tp[:, :-1], tp[:, 1:]], axis=2)

    kb, vb = band(k), band(v)
    s = jnp.einsum('bnqkgd,bnskd->bnkgqs', q, kb).astype(jnp.float32) * (ATT_HEAD_DIM ** -0.5)
    s = jnp.where(band_mask[None, :, None, None], s + band_bias, NEG)
    sink = sinks.astype(jnp.float32).reshape(ATT_KV_HEADS, ATT_GROUP, 1, 1)
    m = jnp.maximum(jnp.max(s, axis=-1, keepdims=True), sink)
    e = jnp.exp(s - m)
    p = e / (jnp.sum(e, axis=-1, keepdims=True) + jnp.exp(sink - m))
    o = jnp.einsum('bnkgqs,bnskd->bnqkgd', p.astype(v.dtype), vb)
    return o.reshape(bsz, L, ATT_Q_HEADS * ATT_HEAD_DIM)


def setup_inputs(seed: int = 0) -> dict:
    key = jax.random.key(seed)
    ks = iter(jax.random.split(key, 32))
    f32 = jnp.float32

    def nrm(shape, fan_in):
        return jax.random.normal(next(ks), shape, f32) * (fan_in ** -0.5)

    def gain(shape):
        return 1.0 + 0.1 * jax.random.normal(next(ks), shape, f32)

    x = jax.random.normal(next(ks), (BATCH, SEQ, D_MODEL), f32)
    ffn1_pre_g = gain((DEPTH, D_MODEL))
    ffn1_post_g = gain((DEPTH, D_MODEL))
    w_ffn1_in = nrm((DEPTH, D_MODEL, 2 * D_FF), D_MODEL)
    w_ffn1_out = nrm((DEPTH, D_FF, D_MODEL), D_FF)
    mix_pre_g = gain((DEPTH, D_MODEL))
    mix_post_g = gain((DEPTH, D_MODEL))
    w_in = nrm((DEPTH, D_MODEL, IN_WIDTH), D_MODEL)
    conv_w = nrm((DEPTH, SSD_CONV, SSD_CONV_DIM), SSD_CONV)
    conv_b = 0.1 * jax.random.normal(next(ks), (DEPTH, SSD_CONV_DIM), f32)
    dt0 = jnp.exp(jax.random.uniform(next(ks), (DEPTH, SSD_HEADS), f32,
                                     math.log(1e-3), math.log(1e-1)))
    dt_bias = dt0 + jnp.log(-jnp.expm1(-dt0))
    a_log = jnp.log(jax.random.uniform(next(ks), (DEPTH, SSD_HEADS), f32, 1.0, 16.0))
    d_skip = gain((DEPTH, SSD_HEADS))
    ssd_norm_g = gain((DEPTH, SSD_D_INNER))
    ret_gn_g = gain((DEPTH, RET_HEADS * RET_HEAD_V))
    attn_sinks = 0.5 * jax.random.normal(next(ks), (DEPTH, ATT_Q_HEADS), f32)
    rel_bias = 0.5 * jax.random.normal(next(ks), (REL_BUCKETS, ATT_Q_HEADS), f32)
    b_gate = 0.1 * jax.random.normal(next(ks), (DEPTH, N_BRANCH * D_MODEL), f32)
    w_branch = nrm((DEPTH, N_BRANCH, BRANCH_WIDTH, D_MODEL), BRANCH_WIDTH)
    w_out = nrm((DEPTH, D_MODEL, D_MODEL), D_MODEL)
    ffn2_pre_g = gain((DEPTH, D_MODEL))
    ffn2_post_g = gain((DEPTH, D_MODEL))
    w_ffn2_in = nrm((DEPTH, D_MODEL, 2 * D_FF), D_MODEL)
    w_ffn2_out = nrm((DEPTH, D_FF, D_MODEL), D_FF)
    return {"x": x, "ffn1_pre_g": ffn1_pre_g, "ffn1_post_g": ffn1_post_g,
            "w_ffn1_in": w_ffn1_in, "w_ffn1_out": w_ffn1_out,
            "mix_pre_g": mix_pre_g, "mix_post_g": mix_post_g, "w_in": w_in,
            "conv_w": conv_w, "conv_b": conv_b, "dt_bias": dt_bias, "a_log": a_log,
            "d_skip": d_skip, "ssd_norm_g": ssd_norm_g, "ret_gn_g": ret_gn_g,
            "attn_sinks": attn_sinks, "rel_bias": rel_bias, "b_gate": b_gate,
            "w_branch": w_branch, "w_out": w_out,
            "ffn2_pre_g": ffn2_pre_g, "ffn2_post_g": ffn2_post_g,
            "w_ffn2_in": w_ffn2_in, "w_ffn2_out": w_ffn2_out}


def reference(x, ffn1_pre_g, ffn1_post_g, w_ffn1_in, w_ffn1_out, mix_pre_g, mix_post_g, w_in,
              conv_w, conv_b, dt_bias, a_log, d_skip, ssd_norm_g, ret_gn_g, attn_sinks, rel_bias,
              b_gate, w_branch, w_out, ffn2_pre_g, ffn2_post_g, w_ffn2_in, w_ffn2_out):
    bsz, L, _ = x.shape
    offsets = np.cumsum(IN_SPLIT_SIZES)[:-1].tolist()
    pos = jnp.arange(L, dtype=jnp.float32)
    inv = 1.0 / (ROPE_BASE ** (jnp.arange(0, RET_QK_DIM, 2, dtype=jnp.float32) / RET_QK_DIM))
    ang = pos[:, None] * inv[None, :]
    cos, sin = jnp.cos(ang), jnp.sin(ang)
    nb = L // ATT_BLOCK
    qi = jnp.arange(ATT_BLOCK)[:, None]
    sj = jnp.arange(2 * ATT_BLOCK)[None, :]
    dist = qi + ATT_BLOCK - sj
    blk = jnp.arange(nb)[:, None, None]
    band_mask = (dist >= 0) & (dist < WINDOW) & ((blk > 0) | (sj >= ATT_BLOCK))
    band_bias = jnp.transpose(rel_bias[t5_bucket(jnp.maximum(dist, 0))], (2, 0, 1))
    band_bias = band_bias.astype(jnp.float32).reshape(ATT_KV_HEADS, ATT_GROUP, ATT_BLOCK, 2 * ATT_BLOCK)

    for l in range(DEPTH):
        x = x + 0.5 * rmsnorm(swiglu(rmsnorm(x, ffn1_pre_g[l]), w_ffn1_in[l], w_ffn1_out[l]), ffn1_post_g[l])
        h = rmsnorm(x, mix_pre_g[l])
        (z, xbc, dt_raw, rq, rk, rv, rg, aq, ak, av, gate) = jnp.split(h @ w_in[l], offsets, axis=-1)
        y_ssd = ssd_mixer(z, xbc, dt_raw, conv_w[l], conv_b[l], dt_bias[l], a_log[l], d_skip[l], ssd_norm_g[l])
        y_ret = retention_mixer(rq, rk, rv, rg, ret_gn_g[l], cos, sin)
        y_att = swa_sink_attention(aq, ak, av, attn_sinks[l], band_bias, band_mask)
        ys = jnp.stack([y_ssd.astype(x.dtype), y_ret.astype(x.dtype), y_att.astype(x.dtype)], axis=2)
        branch = jnp.einsum('blmc,mcd->blmd', ys, w_branch[l])
        gates = jax.nn.sigmoid((gate + b_gate[l]).reshape(bsz, L, N_BRANCH, D_MODEL))
        y = jnp.sum(gates * branch, axis=2) @ w_out[l]
        x = x + rmsnorm(y, mix_post_g[l])
        x = x + 0.5 * rmsnorm(swiglu(rmsnorm(x, ffn2_pre_g[l]), w_ffn2_in[l], w_ffn2_out[l]), ffn2_post_g[l])
    return x
```

```python
import functools
import math

import jax
import jax.numpy as jnp
import numpy as np
from jax import lax
from jax.experimental import pallas as pl
from jax.experimental.pallas import tpu as pltpu

F32 = jnp.float32
BF16 = jnp.bfloat16

D_MODEL = 1024
DEPTH = 4
D_FF = 2816
EPS = 1e-6
NEG = -1e30
CHUNK = 128

SSD_HEADS = 16
SSD_HEADDIM = 64
SSD_GROUPS = 2
SSD_STATE = 128
SSD_CONV = 4
SSD_D_INNER = 1024
SSD_CONV_DIM = SSD_D_INNER + 2 * SSD_GROUPS * SSD_STATE
RET_HEADS = 4
RET_QK_DIM = 128
RET_HEAD_V = 256
ROPE_BASE = 10000.0
ATT_Q_HEADS = 16
ATT_KV_HEADS = 2
ATT_HEAD_DIM = 64
WINDOW = 128
REL_BUCKETS = 32
REL_MAX_EXACT = 16
REL_MAX_DIST = 128
N_BRANCH = 3

IN_SPLIT_SIZES = (1024, SSD_CONV_DIM, SSD_HEADS, 512, 512, 1024, 1024, 1024, 128, 128, N_BRANCH * D_MODEL)

Z0 = 0
XBC0 = Z0 + 1024
RQ0 = XBC0 + SSD_CONV_DIM
RK0 = RQ0 + 512
RV0 = RK0 + 512
RG0 = RV0 + 1024
AQ0 = RG0 + 1024
AKV0 = AQ0 + 1024
PM_W = AKV0 + 256
DT_W = 128
PG_W = N_BRANCH * D_MODEL
PW_DT0 = PM_W
PW_G0 = PM_W + DT_W
PW_W = PW_G0 + PG_W

V7X_VMEM_LIMIT = 56 * 1024 * 1024

NT_DIMS = (((1,), (1,)), ((), ()))
TN_DIMS = (((0,), (0,)), ((), ()))


def _dot(a, b):
    return jnp.dot(a, b, preferred_element_type=F32)


def _rms(x, g):
    ms = jnp.mean(x * x, axis=-1, keepdims=True)
    return x * lax.rsqrt(ms + EPS) * g


def _silu(x):
    return x * jax.nn.sigmoid(x)


def _split_bf16(x, n):
    parts = []
    r = x
    for i in range(n):
        p = r.astype(BF16)
        parts.append(p)
        if i + 1 < n:
            r = r - p.astype(F32)
    return parts


def _ffn_kernel(x_ref, pre_ref, post_ref, wi_ref, wo_ref, o_ref):
    x = x_ref[...]
    hb = _rms(x, pre_ref[...]).astype(BF16)
    ab = _dot(hb, wi_ref[...])
    g = (_silu(ab[:, :D_FF]) * ab[:, D_FF:]).astype(BF16)
    y = _dot(g, wo_ref[...])
    o_ref[...] = x + 0.5 * _rms(y, post_ref[...])


def _resident(shape, layer):
    nd = len(shape)
    return pl.BlockSpec((None,) + tuple(shape), lambda *_: (layer,) + (0,) * nd,
                        pipeline_mode=pl.Buffered(1))


def _row(layer, width):
    return pl.BlockSpec((None, 1, width), lambda *_: (layer, 0, 0))


def _ffn(x, pre_g, post_g, wi, wo, layer, tm):
    t = x.shape[0]
    return pl.pallas_call(
        _ffn_kernel,
        grid=(t // tm,),
        in_specs=[
            pl.BlockSpec((tm, D_MODEL), lambda i: (i, 0)),
            _row(layer, D_MODEL), _row(layer, D_MODEL),
            _resident((D_MODEL, 2 * D_FF), layer),
            _resident((D_FF, D_MODEL), layer),
        ],
        out_specs=pl.BlockSpec((tm, D_MODEL), lambda i: (i, 0)),
        out_shape=jax.ShapeDtypeStruct((t, D_MODEL), F32),
        compiler_params=pltpu.CompilerParams(
            dimension_semantics=("arbitrary",), vmem_limit_bytes=V7X_VMEM_LIMIT),
        name="ffn",
    )(x, pre_g, post_g, wi, wo)


_PM_CHUNKS = tuple((c, min(c + 1024, PM_W)) for c in range(0, PM_W, 1024))


def _inproj_kernel(x_ref, g_ref, w_ref, pm_ref, pdt_ref, pg_ref):
    hb = _rms(x_ref[...], g_ref[...]).astype(BF16)
    for c0, c1 in _PM_CHUNKS:
        pm_ref[:, c0:c1] = _dot(hb, w_ref[:, c0:c1]).astype(BF16)
    pdt_ref[...] = _dot(hb, w_ref[:, PW_DT0:PW_G0])
    for m in range(N_BRANCH):
        c0 = PW_G0 + m * D_MODEL
        pg_ref[:, m * D_MODEL:(m + 1) * D_MODEL] = _dot(hb, w_ref[:, c0:c0 + D_MODEL]).astype(BF16)


def _inproj(x, g, w, layer, tm):
    t = x.shape[0]
    return pl.pallas_call(
        _inproj_kernel,
        grid=(t // tm,),
        in_specs=[
            pl.BlockSpec((tm, D_MODEL), lambda i: (i, 0)),
            _row(layer, D_MODEL),
            _resident((D_MODEL, PW_W), layer),
        ],
        out_specs=[
            pl.BlockSpec((tm, PM_W), lambda i: (i, 0)),
            pl.BlockSpec((tm, DT_W), lambda i: (i, 0)),
            pl.BlockSpec((tm, PG_W), lambda i: (i, 0)),
        ],
        out_shape=[
            jax.ShapeDtypeStruct((t, PM_W), BF16),
            jax.ShapeDtypeStruct((t, DT_W), F32),
            jax.ShapeDtypeStruct((t, PG_W), BF16),
        ],
        compiler_params=pltpu.CompilerParams(
            dimension_semantics=("arbitrary",), vmem_limit_bytes=V7X_VMEM_LIMIT),
        name="inproj",
    )(x, g, w)


_RET_LOG_G = [math.log(1.0 - 2.0 ** (-5.0 - h)) for h in range(RET_HEADS)]


def _mixer_kernel(sink_ref, pm_ref, pdt_ref, bias_ref, cos_ref, sin_ref,
                  convw_ref, convb_ref, dtb_ref, arow_ref, dskip_ref, ssdg_ref, retg_ref,
                  dmat_ref, zeta_ref, xi_ref, rdec_ref, expand_ref,
                  ys_ref,
                  conv_buf, kvprev, sstate, rstate):
    c = pl.program_id(1)

    @pl.when(c == 0)
    def _():
        conv_buf[0:8, :] = jnp.zeros((8, SSD_CONV_DIM), F32)
        kvprev[...] = jnp.zeros_like(kvprev)
        sstate[...] = jnp.zeros_like(sstate)
        rstate[...] = jnp.zeros_like(rstate)

    row = lax.broadcasted_iota(jnp.int32, (CHUNK, CHUNK), 0)
    lane = lax.broadcasted_iota(jnp.int32, (CHUNK, CHUNK), 1)
    causal = row >= lane
    lo_half = lane < 64

    conv_buf[8:8 + CHUNK, :] = pm_ref[:, XBC0:XBC0 + SSD_CONV_DIM].astype(F32)
    acc = convw_ref[0:1, :] * conv_buf[5:5 + CHUNK, :]
    for k in range(1, SSD_CONV):
        acc = acc + convw_ref[k:k + 1, :] * conv_buf[5 + k:5 + k + CHUNK, :]
    acc = acc + convb_ref[...]
    conv_buf[0:8, :] = conv_buf[CHUNK:CHUNK + 8, :]
    xbc = _silu(acc)
    xs = xbc[:, 0:SSD_D_INNER]
    bm = xbc[:, SSD_D_INNER:SSD_D_INNER + 256]
    cm = xbc[:, SSD_D_INNER + 256:SSD_D_INNER + 512]

    v = pdt_ref[...] + dtb_ref[...]
    dt = jnp.maximum(v, 0.0) + jnp.log1p(jnp.exp(-jnp.abs(v)))
    da = dt * arow_ref[...]
    tril = jnp.where(causal, 1.0, 0.0).astype(BF16)
    a_cs = sum(_dot(tril, p) for p in _split_bf16(da, 3))
    a_cs_t = a_cs.T
    dec_st = jnp.exp(a_cs[CHUNK - 1:CHUNK, :] - a_cs)
    dec_out = jnp.exp(a_cs)
    stack = jnp.concatenate([dt, dec_st, dec_out], axis=0)
    ex = sum(_dot(p, expand_ref[...]) for p in _split_bf16(stack, 2))
    dt_e = ex[0:CHUNK]
    dst_e = ex[CHUNK:2 * CHUNK]
    dout_e = ex[2 * CHUNK:3 * CHUNK]

    xdt = xs * dt_e
    xd = (xdt * dst_e).astype(BF16)
    cdec = dout_e[CHUNK - 1:CHUNK, :]

    ydiag = []
    yoff = []
    for g in range(SSD_GROUPS):
        bmg = bm[:, g * 128:(g + 1) * 128].astype(BF16)
        cmg = cm[:, g * 128:(g + 1) * 128].astype(BF16)
        cb = lax.dot_general(cmg, bmg, NT_DIMS, preferred_element_type=F32)
        for jj in range(4):
            j = g * 4 + jj
            lm = []
            for h in (2 * j, 2 * j + 1):
                seg = a_cs[:, h:h + 1] - a_cs_t[h:h + 1, :]
                lm.append(cb * jnp.exp(jnp.where(causal, seg, -jnp.inf)))
            mcat = jnp.concatenate(lm, axis=1).astype(BF16)
            xp = xdt[:, j * 128:(j + 1) * 128]
            x2 = jnp.concatenate([jnp.where(lo_half, xp, 0.0), jnp.where(lo_half, 0.0, xp)],
                                 axis=0).astype(BF16)
            ydiag.append(_dot(mcat, x2))
        new_t = lax.dot_general(bmg, xd[:, g * 512:(g + 1) * 512], TN_DIMS,
                                preferred_element_type=F32)
        prev_t = sstate[g]
        yoff.append(_dot(cmg, prev_t.astype(BF16)))
        sstate[g] = prev_t * cdec[:, g * 512:(g + 1) * 512] + new_t
    y = (jnp.concatenate(ydiag, axis=1) + jnp.concatenate(yoff, axis=1) * dout_e
         + xs * dskip_ref[...])
    z = pm_ref[:, Z0:Z0 + 1024].astype(F32)
    ys_ref[:, 0:1024] = _rms(y * _silu(z), ssdg_ref[...]).astype(BF16)

    cos2 = cos_ref[...]
    sin2 = sin_ref[...]
    o_parts = []
    for h in range(RET_HEADS):
        qh = pm_ref[:, RQ0 + h * 128:RQ0 + (h + 1) * 128].astype(F32)
        kh = pm_ref[:, RK0 + h * 128:RK0 + (h + 1) * 128].astype(F32)
        qr = qh * cos2 + pltpu.roll(qh, 64, 1) * sin2
        kr = (kh * cos2 + pltpu.roll(kh, 64, 1) * sin2) * (RET_QK_DIM ** -0.5)
        vh = pm_ref[:, RV0 + h * 256:RV0 + (h + 1) * 256]
        qb = qr.astype(BF16)
        s = lax.dot_general(qb, kr.astype(BF16), NT_DIMS, preferred_element_type=F32) * dmat_ref[h]
        inner = _dot(s.astype(BF16), vh)
        kz = (kr * zeta_ref[h]).astype(BF16)
        new_kv = lax.dot_general(kz, vh, TN_DIMS, preferred_element_type=F32)
        prev = rstate[h]
        cross = _dot(qb, prev.astype(BF16)) * xi_ref[h]
        rstate[h] = prev * rdec_ref[h] + new_kv
        o = inner + cross
        mu = jnp.mean(o, axis=-1, keepdims=True)
        d = o - mu
        var = jnp.mean(d * d, axis=-1, keepdims=True)
        o_parts.append(d * lax.rsqrt(var + EPS))
    o = jnp.concatenate(o_parts, axis=1) * retg_ref[...]
    rg = pm_ref[:, RG0:RG0 + 1024].astype(F32)
    ys_ref[:, 1024:2048] = (o * _silu(rg)).astype(BF16)

    bias_sel = jnp.minimum(c, 1)
    kvc = pm_ref[:, AKV0:AKV0 + 256].astype(F32)
    kv = jnp.concatenate([kvprev[...], kvc], axis=0)
    kvprev[...] = kvc
    lane2 = lax.broadcasted_iota(jnp.int32, (2 * CHUNK, 128), 1)
    lo2 = lane2 < 64
    kblk = kv[:, 0:128]
    vblk = kv[:, 128:256]
    kblk_r = pltpu.roll(kblk, 64, 1)
    vblk_r = pltpu.roll(vblk, 64, 1)
    for kvh in range(ATT_KV_HEADS):
        k_lo, k_hi = (kblk, kblk_r) if kvh == 0 else (kblk_r, kblk)
        v_lo, v_hi = (vblk, vblk_r) if kvh == 0 else (vblk_r, vblk)
        k2 = jnp.concatenate([jnp.where(lo2, k_lo, 0.0), jnp.where(lo2, 0.0, k_hi)],
                             axis=0).astype(BF16)
        v2 = jnp.concatenate([jnp.where(lo2, v_lo, 0.0), jnp.where(lo2, 0.0, v_hi)],
                             axis=0).astype(BF16)
        for j in range(4):
            jp = kvh * 4 + j
            qp = pm_ref[:, AQ0 + jp * 128:AQ0 + (jp + 1) * 128]
            s = (lax.dot_general(qp, k2, NT_DIMS, preferred_element_type=F32)
                 * (ATT_HEAD_DIM ** -0.5) + bias_ref[bias_sel, jp])
            ps = []
            for t in range(2):
                st = s[:, t * 256:(t + 1) * 256]
                snk = sink_ref[2 * jp + t]
                m = jnp.maximum(jnp.max(st, axis=-1, keepdims=True), snk)
                e = jnp.exp(st - m)
                den = jnp.sum(e, axis=-1, keepdims=True) + jnp.exp(snk - m)
                ps.append(e * (1.0 / den))
            pcat = jnp.concatenate(ps, axis=1).astype(BF16)
            ys_ref[:, 2048 + jp * 128:2048 + (jp + 1) * 128] = _dot(pcat, v2).astype(BF16)


def _const(shape):
    nd = len(shape)
    return pl.BlockSpec(tuple(shape), lambda *_: (0,) * nd)


def _layer_const(shape, layer):
    nd = len(shape)
    return pl.BlockSpec((None,) + tuple(shape), lambda *_: (layer,) + (0,) * nd)


def _mixer(pm, pdt, sinks, tabs, lay, layer, bsz, nc):
    t = pm.shape[0]
    tok = lambda w: pl.BlockSpec((CHUNK, w), lambda b, c: (b * nc + c, 0))
    return pl.pallas_call(
        _mixer_kernel,
        grid=(bsz, nc),
        in_specs=[
            pl.BlockSpec(memory_space=pltpu.SMEM),
            tok(PM_W), tok(DT_W),
            pl.BlockSpec((2, 8, CHUNK, 512), lambda b, c: (0, 0, 0, 0), pipeline_mode=pl.Buffered(1)),
            pl.BlockSpec((CHUNK, 128), lambda b, c: (c, 0)),
            pl.BlockSpec((CHUNK, 128), lambda b, c: (c, 0)),
            _layer_const((SSD_CONV, SSD_CONV_DIM), layer),
            _layer_const((1, SSD_CONV_DIM), layer),
            _layer_const((1, 128), layer),
            _layer_const((1, 128), layer),
            _layer_const((1, 1024), layer),
            _layer_const((1, 1024), layer),
            _layer_const((1, 1024), layer),
            _const((RET_HEADS, CHUNK, CHUNK)),
            _const((RET_HEADS, CHUNK, 128)),
            _const((RET_HEADS, CHUNK, RET_HEAD_V)),
            _const((RET_HEADS, 1, RET_HEAD_V)),
            _const((128, 1024)),
        ],
        out_specs=pl.BlockSpec((CHUNK, 3 * D_MODEL), lambda b, c: (b * nc + c, 0)),
        out_shape=jax.ShapeDtypeStruct((t, 3 * D_MODEL), BF16),
        scratch_shapes=[
            pltpu.VMEM((CHUNK + 8, SSD_CONV_DIM), F32),
            pltpu.VMEM((CHUNK, 256), F32),
            pltpu.VMEM((SSD_GROUPS, SSD_STATE, 512), F32),
            pltpu.VMEM((RET_HEADS, RET_QK_DIM, RET_HEAD_V), F32),
        ],
        compiler_params=pltpu.CompilerParams(
            dimension_semantics=("arbitrary", "arbitrary"), vmem_limit_bytes=V7X_VMEM_LIMIT),
        name="mixer",
    )(sinks, pm, pdt, tabs["bias"], tabs["cos2"], tabs["sin2"],
      lay["conv_w"], lay["conv_b"], lay["dt_bias"], lay["a_row"], lay["d_skip"],
      lay["ssd_g"], lay["ret_g"],
      tabs["dmat"], tabs["zeta"], tabs["xi"], tabs["rdec"], tabs["expand"])


def _merge_kernel(x_ref, ys_ref, pg_ref, bg_ref, post_ref, wb_ref, wo_ref, o_ref):
    acc = None
    for m in range(N_BRANCH):
        sl = slice(m * D_MODEL, (m + 1) * D_MODEL)
        br = _dot(ys_ref[:, sl], wb_ref[m])
        gt = jax.nn.sigmoid(pg_ref[:, sl].astype(F32) + bg_ref[:, sl])
        acc = gt * br if acc is None else acc + gt * br
    y = _dot(acc.astype(BF16), wo_ref[...])
    o_ref[...] = x_ref[...] + _rms(y, post_ref[...])


def _merge(x, ys, pg, bg, post_g, wb, wo, layer, tm):
    t = x.shape[0]
    return pl.pallas_call(
        _merge_kernel,
        grid=(t // tm,),
        in_specs=[
            pl.BlockSpec((tm, D_MODEL), lambda i: (i, 0)),
            pl.BlockSpec((tm, 3 * D_MODEL), lambda i: (i, 0)),
            pl.BlockSpec((tm, PG_W), lambda i: (i, 0)),
            _row(layer, PG_W), _row(layer, D_MODEL),
            _resident((N_BRANCH, D_MODEL, D_MODEL), layer),
            _resident((D_MODEL, D_MODEL), layer),
        ],
        out_specs=pl.BlockSpec((tm, D_MODEL), lambda i: (i, 0)),
        out_shape=jax.ShapeDtypeStruct((t, D_MODEL), F32),
        compiler_params=pltpu.CompilerParams(
            dimension_semantics=("arbitrary",), vmem_limit_bytes=V7X_VMEM_LIMIT),
        name="merge",
    )(x, ys, pg, bg, post_g, wb, wo)


def _t5_bucket(dist):
    is_small = dist < REL_MAX_EXACT
    d = jnp.maximum(dist, 1).astype(F32)
    large = REL_MAX_EXACT + (jnp.log(d / REL_MAX_EXACT) / math.log(REL_MAX_DIST / REL_MAX_EXACT)
                             * (REL_BUCKETS - REL_MAX_EXACT)).astype(jnp.int32)
    large = jnp.minimum(large, REL_BUCKETS - 1)
    return jnp.where(is_small, dist, large)


def _tables(seq, rel_bias):
    pos = jnp.arange(seq, dtype=F32)
    inv = 1.0 / (ROPE_BASE ** (jnp.arange(0, RET_QK_DIM, 2, dtype=F32) / RET_QK_DIM))
    ang = pos[:, None] * inv[None, :]
    cos, sin = jnp.cos(ang), jnp.sin(ang)
    cos2 = jnp.concatenate([cos, cos], axis=-1)
    sin2 = jnp.concatenate([-sin, sin], axis=-1)

    log_g = jnp.log(1.0 - jnp.exp2(-5.0 - jnp.arange(RET_HEADS, dtype=F32)))
    idx = jnp.arange(CHUNK, dtype=F32)
    diff = idx[:, None] - idx[None, :]
    dmat = jnp.where(diff >= 0, jnp.exp(jnp.maximum(diff, 0.0)[None] * log_g[:, None, None]), 0.0)
    zeta = jnp.exp((CHUNK - 1.0 - idx)[None, :] * log_g[:, None])
    xi = jnp.exp((idx + 1.0)[None, :] * log_g[:, None])
    rdec = jnp.exp(CHUNK * log_g)
    zeta_t = jnp.broadcast_to(zeta[:, :, None], (RET_HEADS, CHUNK, 128))
    xi_t = jnp.broadcast_to(xi[:, :, None], (RET_HEADS, CHUNK, RET_HEAD_V))
    rdec_t = jnp.broadcast_to(rdec[:, None, None], (RET_HEADS, 1, RET_HEAD_V))

    qi = jnp.arange(CHUNK)[:, None]
    sj = jnp.arange(2 * CHUNK)[None, :]
    dist = qi + CHUNK - sj
    in_band = (dist >= 0) & (dist < WINDOW)
    bias = jnp.transpose(rel_bias[_t5_bucket(jnp.maximum(dist, 0))], (2, 0, 1)).astype(F32)
    first = jnp.where(in_band & (sj >= CHUNK), bias, NEG)
    rest = jnp.where(in_band, bias, NEG)
    pair = lambda b: b.reshape(ATT_Q_HEADS // 2, 2, CHUNK, 2 * CHUNK).transpose(0, 2, 1, 3).reshape(
        ATT_Q_HEADS // 2, CHUNK, 4 * CHUNK)
    bias_t = jnp.stack([pair(first), pair(rest)], axis=0)

    expand = np.zeros((128, SSD_D_INNER), np.float32)
    for h in range(SSD_HEADS):
        expand[h, h * SSD_HEADDIM:(h + 1) * SSD_HEADDIM] = 1.0
    return dict(cos2=cos2, sin2=sin2, dmat=dmat, zeta=zeta_t, xi=xi_t, rdec=rdec_t, bias=bias_t,
                expand=jnp.asarray(expand, BF16))


def _regroup_w_in(w_in):
    offs = np.concatenate([[0], np.cumsum(IN_SPLIT_SIZES)])
    seg = [w_in[:, :, offs[i]:offs[i + 1]] for i in range(len(IN_SPLIT_SIZES))]
    z, xbc, dt, rq, rk, rv, rg, aq, ak, av, gate = seg
    dt = jnp.pad(dt, ((0, 0), (0, 0), (0, DT_W - SSD_HEADS)))
    return jnp.concatenate([z, xbc, rq, rk, rv, rg, aq, ak, av, dt, gate], axis=-1).astype(BF16)


def kernel(x, ffn1_pre_g, ffn1_post_g, w_ffn1_in, w_ffn1_out, mix_pre_g, mix_post_g, w_in, conv_w, conv_b, dt_bias, a_log, d_skip, ssd_norm_g, ret_gn_g, attn_sinks, rel_bias, b_gate, w_branch, w_out, ffn2_pre_g, ffn2_post_g, w_ffn2_in, w_ffn2_out):
    bsz, seq, _ = x.shape
    assert seq % CHUNK == 0
    t = bsz * seq
    nc = seq // CHUNK
    tm_ffn = min(512, t)
    tm_proj = min(256, t)

    tabs = _tables(seq, rel_bias)
    row3 = lambda a: a.astype(F32)[:, None, :]
    pad16 = lambda a: jnp.pad(a.astype(F32), ((0, 0), (0, 128 - SSD_HEADS)))[:, None, :]
    lay = dict(
        conv_w=conv_w.astype(F32), conv_b=row3(conv_b),
        dt_bias=pad16(dt_bias), a_row=pad16(-jnp.exp(a_log.astype(F32))),
        d_skip=row3(jnp.repeat(d_skip, SSD_HEADDIM, axis=-1)),
        ssd_g=row3(ssd_norm_g), ret_g=row3(ret_gn_g))
    wi1, wo1 = w_ffn1_in.astype(BF16), w_ffn1_out.astype(BF16)
    wi2, wo2 = w_ffn2_in.astype(BF16), w_ffn2_out.astype(BF16)
    w_in_r = _regroup_w_in(w_in)
    wb, wo = w_branch.astype(BF16), w_out.astype(BF16)
    g1a, g1b = row3(ffn1_pre_g), row3(ffn1_post_g)
    g2a, g2b = row3(ffn2_pre_g), row3(ffn2_post_g)
    gma, gmb = row3(mix_pre_g), row3(mix_post_g)
    bg = row3(b_gate)
    sinks = attn_sinks.astype(F32)

    xf = x.reshape(t, D_MODEL)
    for l in range(DEPTH):
        xf = _ffn(xf, g1a, g1b, wi1, wo1, l, tm_ffn)
        pm, pdt, pg = _inproj(xf, gma, w_in_r, l, tm_proj)
        ys = _mixer(pm, pdt, sinks[l], tabs, lay, l, bsz, nc)
        xf = _merge(xf, ys, pg, bg, gmb, wb, wo, l, tm_ffn)
        xf = _ffn(xf, g2a, g2b, wi2, wo2, l, tm_ffn)
    return xf.reshape(bsz, seq, D_MODEL)
```

```python
import math

import jax
import jax.numpy as jnp
import numpy as np
from jax import lax
from jax.experimental import pallas as pl
from jax.experimental.pallas import tpu as pltpu

F32 = jnp.float32
BF16 = jnp.bfloat16

D_MODEL = 1024
DEPTH = 4
D_FF = 2816
EPS = 1e-6
NEG = -1e30
CHUNK = 128

SSD_HEADS = 16
SSD_HEADDIM = 64
SSD_GROUPS = 2
SSD_STATE = 128
SSD_CONV = 4
SSD_D_INNER = 1024
SSD_CONV_DIM = SSD_D_INNER + 2 * SSD_GROUPS * SSD_STATE
RET_HEADS = 4
RET_QK_DIM = 128
RET_HEAD_V = 256
ROPE_BASE = 10000.0
ATT_Q_HEADS = 16
ATT_KV_HEADS = 2
ATT_HEAD_DIM = 64
WINDOW = 128
REL_BUCKETS = 32
REL_MAX_EXACT = 16
REL_MAX_DIST = 128
N_BRANCH = 3

IN_SPLIT_SIZES = (1024, SSD_CONV_DIM, SSD_HEADS, 512, 512, 1024, 1024, 1024, 128, 128, N_BRANCH * D_MODEL)

WX0 = 0
WQ0 = WX0 + SSD_CONV_DIM
WK0 = WQ0 + 512
WV0 = WK0 + 512
WAQ0 = WV0 + 1024
WAKV0 = WAQ0 + 1024
WZ0 = WAKV0 + 256
WDT0 = WZ0 + 5 * 1024
DT_W = 128
PW_W = WDT0 + DT_W

MX0 = 0
MQ0 = MX0 + SSD_CONV_DIM
MK0 = MQ0 + 512
MKZ0 = MK0 + 512
MV0 = MKZ0 + 512
MAQ0 = MV0 + 1024
MAKV0 = MAQ0 + 1024
PM_W = MAKV0 + 256
PZ_W = 5 * 1024

V7X_VMEM_LIMIT = 56 * 1024 * 1024

NT_DIMS = (((1,), (1,)), ((), ()))
TN_DIMS = (((0,), (0,)), ((), ()))


def _dot(a, b):
    return jnp.dot(a, b, preferred_element_type=F32)


def _rms(x, g):
    ms = jnp.mean(x * x, axis=-1, keepdims=True)
    return x * lax.rsqrt(ms + EPS) * g


def _silu(x):
    return x * jax.nn.sigmoid(x)


def _split_bf16(x, n):
    parts = []
    r = x
    for i in range(n):
        p = r.astype(BF16)
        parts.append(p)
        if i + 1 < n:
            r = r - p.astype(F32)
    return parts


def _resident(shape, layer):
    nd = len(shape)
    return pl.BlockSpec((None,) + tuple(shape), lambda *_: (layer,) + (0,) * nd,
                        pipeline_mode=pl.Buffered(1))


def _layer_const(shape, layer):
    nd = len(shape)
    return pl.BlockSpec((None,) + tuple(shape), lambda *_: (layer,) + (0,) * nd)


def _const(shape):
    nd = len(shape)
    return pl.BlockSpec(tuple(shape), lambda *_: (0,) * nd)


def _row(layer, width):
    return _layer_const((1, width), layer)


def _params(n_axes):
    return pltpu.CompilerParams(dimension_semantics=("arbitrary",) * n_axes,
                                vmem_limit_bytes=V7X_VMEM_LIMIT)


def _ffn_kernel(x_ref, pre_ref, post_ref, wi_ref, wo_ref, o_ref):
    x = x_ref[...]
    hb = _rms(x, pre_ref[...]).astype(BF16)
    ab = _dot(hb, wi_ref[...])
    g = (_silu(ab[:, :D_FF]) * ab[:, D_FF:]).astype(BF16)
    y = _dot(g, wo_ref[...])
    o_ref[...] = x + 0.5 * _rms(y, post_ref[...])


def _ffn(x, pre_g, post_g, wi, wo, layer, tm):
    t = x.shape[0]
    return pl.pallas_call(
        _ffn_kernel,
        grid=(t // tm,),
        in_specs=[
            pl.BlockSpec((tm, D_MODEL), lambda i: (i, 0)),
            _row(layer, D_MODEL), _row(layer, D_MODEL),
            _resident((D_MODEL, 2 * D_FF), layer),
            _resident((D_FF, D_MODEL), layer),
        ],
        out_specs=pl.BlockSpec((tm, D_MODEL), lambda i: (i, 0)),
        out_shape=jax.ShapeDtypeStruct((t, D_MODEL), F32),
        compiler_params=_params(1),
        name="ffn",
    )(x, pre_g, post_g, wi, wo)


def _inproj_kernel(x_ref, g_ref, w_ref, convw_ref, convb_ref, dtb_ref, cos_ref, sin_ref, zeta_ref,
                   pm_ref, pz_ref, dt_ref, conv_buf, *, tm, tiles_per_seq):
    i = pl.program_id(0)

    @pl.when(i % tiles_per_seq == 0)
    def _():
        conv_buf[0:8, :] = jnp.zeros((8, SSD_CONV_DIM), F32)

    hb = _rms(x_ref[...], g_ref[...]).astype(BF16)

    conv_buf[8:8 + tm, :] = _dot(hb, w_ref[:, WX0:WX0 + SSD_CONV_DIM])
    acc = convw_ref[0:1, :] * conv_buf[5:5 + tm, :]
    for k in range(1, SSD_CONV):
        acc = acc + convw_ref[k:k + 1, :] * conv_buf[5 + k:5 + k + tm, :]
    acc = acc + convb_ref[...]
    conv_buf[0:8, :] = conv_buf[tm:tm + 8, :]
    pm_ref[:, MX0:MX0 + SSD_CONV_DIM] = _silu(acc).astype(BF16)

    q = _dot(hb, w_ref[:, WQ0:WQ0 + 512])
    k = _dot(hb, w_ref[:, WK0:WK0 + 512])
    cos2 = cos_ref[...]
    sin2 = sin_ref[...]
    for h in range(RET_HEADS):
        sl = slice(h * 128, (h + 1) * 128)
        qh = q[:, sl]
        kh = k[:, sl]
        qr = qh * cos2 + pltpu.roll(qh, 64, 1) * sin2
        kr = (kh * cos2 + pltpu.roll(kh, 64, 1) * sin2) * (RET_QK_DIM ** -0.5)
        pm_ref[:, MQ0 + h * 128:MQ0 + (h + 1) * 128] = qr.astype(BF16)
        pm_ref[:, MK0 + h * 128:MK0 + (h + 1) * 128] = kr.astype(BF16)
        pm_ref[:, MKZ0 + h * 128:MKZ0 + (h + 1) * 128] = (kr * zeta_ref[h]).astype(BF16)

    pm_ref[:, MV0:MV0 + 1024] = _dot(hb, w_ref[:, WV0:WV0 + 1024]).astype(BF16)
    pm_ref[:, MAQ0:MAQ0 + 1024] = _dot(hb, w_ref[:, WAQ0:WAQ0 + 1024]).astype(BF16)
    pm_ref[:, MAKV0:MAKV0 + 256] = _dot(hb, w_ref[:, WAKV0:WAKV0 + 256]).astype(BF16)
    for m in range(5):
        pz_ref[:, m * 1024:(m + 1) * 1024] = _dot(
            hb, w_ref[:, WZ0 + m * 1024:WZ0 + (m + 1) * 1024]).astype(BF16)

    v = _dot(hb, w_ref[:, WDT0:WDT0 + DT_W]) + dtb_ref[...]
    dt_ref[...] = jnp.maximum(v, 0.0) + jnp.log1p(jnp.exp(-jnp.abs(v)))


def _inproj(x, g, w, lay, tabs, layer, tm, seq):
    t = x.shape[0]
    tiles_per_seq = seq // tm
    kern = lambda *refs: _inproj_kernel(*refs, tm=tm, tiles_per_seq=tiles_per_seq)
    pos = lambda i: (i % tiles_per_seq, 0)
    return pl.pallas_call(
        kern,
        grid=(t // tm,),
        in_specs=[
            pl.BlockSpec((tm, D_MODEL), lambda i: (i, 0)),
            _row(layer, D_MODEL),
            _resident((D_MODEL, PW_W), layer),
            _layer_const((SSD_CONV, SSD_CONV_DIM), layer),
            _row(layer, SSD_CONV_DIM),
            _row(layer, DT_W),
            pl.BlockSpec((tm, 128), pos),
            pl.BlockSpec((tm, 128), pos),
            _const((RET_HEADS, tm, 128)),
        ],
        out_specs=[
            pl.BlockSpec((tm, PM_W), lambda i: (i, 0)),
            pl.BlockSpec((tm, PZ_W), lambda i: (i, 0)),
            pl.BlockSpec((tm, DT_W), lambda i: (i, 0)),
        ],
        out_shape=[
            jax.ShapeDtypeStruct((t, PM_W), BF16),
            jax.ShapeDtypeStruct((t, PZ_W), BF16),
            jax.ShapeDtypeStruct((t, DT_W), F32),
        ],
        scratch_shapes=[pltpu.VMEM((tm + 8, SSD_CONV_DIM), F32)],
        compiler_params=_params(1),
        name="inproj",
    )(x, g, w, lay["conv_w"], lay["conv_b"], lay["dt_bias"], tabs["cos2"], tabs["sin2"], tabs["zeta"])


def _mixer_kernel(sink_ref, pm_ref, dt_ref, bias_ref, arow_ref, dskip_ref,
                  dmat_ref, xi_ref, rdec_ref, expand_ref,
                  ys_ref,
                  kvprev, sstate, rstate):
    c = pl.program_id(1)

    @pl.when(c == 0)
    def _():
        kvprev[...] = jnp.zeros_like(kvprev)
        sstate[...] = jnp.zeros_like(sstate)
        rstate[...] = jnp.zeros_like(rstate)

    row = lax.broadcasted_iota(jnp.int32, (CHUNK, CHUNK), 0)
    lane = lax.broadcasted_iota(jnp.int32, (CHUNK, CHUNK), 1)
    causal = row >= lane
    lo_half = lane < 64

    dt = dt_ref[...]
    da = dt * arow_ref[...]
    tril = jnp.where(causal, 1.0, 0.0).astype(BF16)
    a_cs = sum(_dot(tril, p) for p in _split_bf16(da, 3))
    a_cs_t = a_cs.T
    dt_t = dt.T
    w_st = dt * jnp.exp(a_cs[CHUNK - 1:CHUNK, :] - a_cs)
    dec_out = jnp.exp(a_cs)
    stack = jnp.concatenate([w_st, dec_out], axis=0)
    ex = sum(_dot(p, expand_ref[...]) for p in _split_bf16(stack, 2))
    wst_e = ex[0:CHUNK]
    dout_e = ex[CHUNK:2 * CHUNK]
    cdec = dout_e[CHUNK - 1:CHUNK, :]

    xs = pm_ref[:, MX0:MX0 + SSD_D_INNER].astype(F32)
    xd = (xs * wst_e).astype(BF16)

    ydiag = []
    yoff = []
    for g in range(SSD_GROUPS):
        bmg = pm_ref[:, MX0 + 1024 + g * 128:MX0 + 1024 + (g + 1) * 128]
        cmg = pm_ref[:, MX0 + 1280 + g * 128:MX0 + 1280 + (g + 1) * 128]
        cb = lax.dot_general(cmg, bmg, NT_DIMS, preferred_element_type=F32)
        for jj in range(4):
            j = g * 4 + jj
            lm = []
            for h in (2 * j, 2 * j + 1):
                seg = a_cs[:, h:h + 1] - a_cs_t[h:h + 1, :]
                lm.append(cb * (jnp.exp(jnp.where(causal, seg, -jnp.inf)) * dt_t[h:h + 1, :]))
            mcat = jnp.concatenate(lm, axis=1).astype(BF16)
            xp = xs[:, j * 128:(j + 1) * 128]
            x2 = jnp.concatenate([jnp.where(lo_half, xp, 0.0), jnp.where(lo_half, 0.0, xp)],
                                 axis=0).astype(BF16)
            ydiag.append(_dot(mcat, x2))
        new_t = lax.dot_general(bmg, xd[:, g * 512:(g + 1) * 512], TN_DIMS,
                                preferred_element_type=F32)
        prev_t = sstate[g]
        yoff.append(_dot(cmg, prev_t.astype(BF16)))
        sstate[g] = prev_t * cdec[:, g * 512:(g + 1) * 512] + new_t
    y = (jnp.concatenate(ydiag, axis=1) + jnp.concatenate(yoff, axis=1) * dout_e
         + xs * dskip_ref[...])
    ys_ref[:, 0:1024] = y.astype(BF16)

    for h in range(RET_HEADS):
        qb = pm_ref[:, MQ0 + h * 128:MQ0 + (h + 1) * 128]
        kb = pm_ref[:, MK0 + h * 128:MK0 + (h + 1) * 128]
        kz = pm_ref[:, MKZ0 + h * 128:MKZ0 + (h + 1) * 128]
        vh = pm_ref[:, MV0 + h * 256:MV0 + (h + 1) * 256]
        s = lax.dot_general(qb, kb, NT_DIMS, preferred_element_type=F32) * dmat_ref[h]
        inner = _dot(s.astype(BF16), vh)
        new_kv = lax.dot_general(kz, vh, TN_DIMS, preferred_element_type=F32)
        prev = rstate[h]
        cross = _dot(qb, prev.astype(BF16)) * xi_ref[h]
        rstate[h] = prev * rdec_ref[h] + new_kv
        ys_ref[:, 1024 + h * 256:1024 + (h + 1) * 256] = (inner + cross).astype(BF16)

    bias_sel = jnp.minimum(c, 1)
    kvc = pm_ref[:, MAKV0:MAKV0 + 256].astype(F32)
    kv = jnp.concatenate([kvprev[...], kvc], axis=0)
    kvprev[...] = kvc
    lane2 = lax.broadcasted_iota(jnp.int32, (2 * CHUNK, 128), 1)
    lo2 = lane2 < 64
    kblk = kv[:, 0:128]
    vblk = kv[:, 128:256]
    kblk_r = pltpu.roll(kblk, 64, 1)
    vblk_r = pltpu.roll(vblk, 64, 1)
    for kvh in range(ATT_KV_HEADS):
        k_lo, k_hi = (kblk, kblk_r) if kvh == 0 else (kblk_r, kblk)
        v_lo, v_hi = (vblk, vblk_r) if kvh == 0 else (vblk_r, vblk)
        k2 = jnp.concatenate([jnp.where(lo2, k_lo, 0.0), jnp.where(lo2, 0.0, k_hi)],
                             axis=0).astype(BF16)
        v2 = jnp.concatenate([jnp.where(lo2, v_lo, 0.0), jnp.where(lo2, 0.0, v_hi)],
                             axis=0).astype(BF16)
        for j in range(4):
            jp = kvh * 4 + j
            qp = pm_ref[:, MAQ0 + jp * 128:MAQ0 + (jp + 1) * 128]
            s = lax.dot_general(qp, k2, NT_DIMS, preferred_element_type=F32) + bias_ref[bias_sel, jp]
            es = []
            rden = []
            for t in range(2):
                st = s[:, t * 256:(t + 1) * 256]
                snk = sink_ref[2 * jp + t]
                m = jnp.maximum(jnp.max(st, axis=-1, keepdims=True), snk)
                e = jnp.exp(st - m)
                den = jnp.sum(e, axis=-1, keepdims=True) + jnp.exp(snk - m)
                es.append(e)
                rden.append(1.0 / den)
            ecat = jnp.concatenate(es, axis=1).astype(BF16)
            o = _dot(ecat, v2) * jnp.where(lo_half, rden[0], rden[1])
            ys_ref[:, 2048 + jp * 128:2048 + (jp + 1) * 128] = o.astype(BF16)


def _mixer(pm, dt, sinks, tabs, lay, layer, bsz, nc):
    t = pm.shape[0]
    tok = lambda w: pl.BlockSpec((CHUNK, w), lambda b, c: (b * nc + c, 0))
    return pl.pallas_call(
        _mixer_kernel,
        grid=(bsz, nc),
        in_specs=[
            pl.BlockSpec(memory_space=pltpu.SMEM),
            tok(PM_W), tok(DT_W),
            pl.BlockSpec((2, 8, CHUNK, 512), lambda b, c: (0, 0, 0, 0), pipeline_mode=pl.Buffered(1)),
            _row(layer, 128),
            _row(layer, 1024),
            _const((RET_HEADS, CHUNK, CHUNK)),
            _const((RET_HEADS, CHUNK, RET_HEAD_V)),
            _const((RET_HEADS, 1, RET_HEAD_V)),
            _const((128, 1024)),
        ],
        out_specs=pl.BlockSpec((CHUNK, 3 * D_MODEL), lambda b, c: (b * nc + c, 0)),
        out_shape=jax.ShapeDtypeStruct((t, 3 * D_MODEL), BF16),
        scratch_shapes=[
            pltpu.VMEM((CHUNK, 256), F32),
            pltpu.VMEM((SSD_GROUPS, SSD_STATE, 512), F32),
            pltpu.VMEM((RET_HEADS, RET_QK_DIM, RET_HEAD_V), F32),
        ],
        compiler_params=_params(2),
        name="mixer",
    )(sinks, pm, dt, tabs["bias"], lay["a_row"], lay["d_skip"],
      tabs["dmat"], tabs["xi"], tabs["rdec"], tabs["expand"])


def _merge_kernel(x_ref, ys_ref, pz_ref, bg_ref, post_ref, ssdg_ref, retg_ref, wb_ref, wo_ref, o_ref):
    z = pz_ref[:, 0:1024].astype(F32)
    y_ssd = _rms(ys_ref[:, 0:1024].astype(F32) * _silu(z), ssdg_ref[...]).astype(BF16)

    o_parts = []
    for h in range(RET_HEADS):
        o = ys_ref[:, 1024 + h * 256:1024 + (h + 1) * 256].astype(F32)
        mu = jnp.mean(o, axis=-1, keepdims=True)
        d = o - mu
        var = jnp.mean(d * d, axis=-1, keepdims=True)
        o_parts.append(d * lax.rsqrt(var + EPS))
    rg = pz_ref[:, 1024:2048].astype(F32)
    y_ret = (jnp.concatenate(o_parts, axis=1) * retg_ref[...] * _silu(rg)).astype(BF16)

    branches = (y_ssd, y_ret, ys_ref[:, 2048:3072])
    acc = None
    for m in range(N_BRANCH):
        br = _dot(branches[m], wb_ref[m])
        gsl = slice(2048 + m * D_MODEL, 2048 + (m + 1) * D_MODEL)
        gt = jax.nn.sigmoid(pz_ref[:, gsl].astype(F32) + bg_ref[:, m * D_MODEL:(m + 1) * D_MODEL])
        acc = gt * br if acc is None else acc + gt * br
    y = _dot(acc.astype(BF16), wo_ref[...])
    o_ref[...] = x_ref[...] + _rms(y, post_ref[...])


def _merge(x, ys, pz, bg, post_g, lay, wb, wo, layer, tm):
    t = x.shape[0]
    return pl.pallas_call(
        _merge_kernel,
        grid=(t // tm,),
        in_specs=[
            pl.BlockSpec((tm, D_MODEL), lambda i: (i, 0)),
            pl.BlockSpec((tm, 3 * D_MODEL), lambda i: (i, 0)),
            pl.BlockSpec((tm, PZ_W), lambda i: (i, 0)),
            _row(layer, N_BRANCH * D_MODEL), _row(layer, D_MODEL),
            _row(layer, D_MODEL), _row(layer, D_MODEL),
            _resident((N_BRANCH, D_MODEL, D_MODEL), layer),
            _resident((D_MODEL, D_MODEL), layer),
        ],
        out_specs=pl.BlockSpec((tm, D_MODEL), lambda i: (i, 0)),
        out_shape=jax.ShapeDtypeStruct((t, D_MODEL), F32),
        compiler_params=_params(1),
        name="merge",
    )(x, ys, pz, bg, post_g, lay["ssd_g"], lay["ret_g"], wb, wo)


def _t5_bucket(dist):
    is_small = dist < REL_MAX_EXACT
    d = jnp.maximum(dist, 1).astype(F32)
    large = REL_MAX_EXACT + (jnp.log(d / REL_MAX_EXACT) / math.log(REL_MAX_DIST / REL_MAX_EXACT)
                             * (REL_BUCKETS - REL_MAX_EXACT)).astype(jnp.int32)
    large = jnp.minimum(large, REL_BUCKETS - 1)
    return jnp.where(is_small, dist, large)


def _tables(seq, rel_bias, tm_proj):
    pos = jnp.arange(seq, dtype=F32)
    inv = 1.0 / (ROPE_BASE ** (jnp.arange(0, RET_QK_DIM, 2, dtype=F32) / RET_QK_DIM))
    ang = pos[:, None] * inv[None, :]
    cos, sin = jnp.cos(ang), jnp.sin(ang)
    cos2 = jnp.concatenate([cos, cos], axis=-1)
    sin2 = jnp.concatenate([-sin, sin], axis=-1)

    log_g = jnp.log(1.0 - jnp.exp2(-5.0 - jnp.arange(RET_HEADS, dtype=F32)))
    idx = jnp.arange(CHUNK, dtype=F32)
    diff = idx[:, None] - idx[None, :]
    dmat = jnp.where(diff >= 0, jnp.exp(jnp.maximum(diff, 0.0)[None] * log_g[:, None, None]), 0.0)
    zeta = jnp.exp((CHUNK - 1.0 - idx)[None, :] * log_g[:, None])
    xi = jnp.exp((idx + 1.0)[None, :] * log_g[:, None])
    rdec = jnp.exp(CHUNK * log_g)
    zeta_t = jnp.broadcast_to(jnp.tile(zeta, (1, tm_proj // CHUNK))[:, :, None],
                              (RET_HEADS, tm_proj, 128))
    xi_t = jnp.broadcast_to(xi[:, :, None], (RET_HEADS, CHUNK, RET_HEAD_V))
    rdec_t = jnp.broadcast_to(rdec[:, None, None], (RET_HEADS, 1, RET_HEAD_V))

    qi = jnp.arange(CHUNK)[:, None]
    sj = jnp.arange(2 * CHUNK)[None, :]
    dist = qi + CHUNK - sj
    in_band = (dist >= 0) & (dist < WINDOW)
    onehot = (_t5_bucket(jnp.maximum(dist, 0))[:, :, None] == jnp.arange(REL_BUCKETS)).astype(F32)
    bias = jnp.einsum('qsb,bh->hqs', onehot, rel_bias.astype(F32), precision=lax.Precision.HIGHEST)
    first = jnp.where(in_band & (sj >= CHUNK), bias, NEG)
    rest = jnp.where(in_band, bias, NEG)
    pair = lambda b: b.reshape(ATT_Q_HEADS // 2, 2, CHUNK, 2 * CHUNK).transpose(0, 2, 1, 3).reshape(
        ATT_Q_HEADS // 2, CHUNK, 4 * CHUNK)
    bias_t = jnp.stack([pair(first), pair(rest)], axis=0)

    expand = np.zeros((128, SSD_D_INNER), np.float32)
    for h in range(SSD_HEADS):
        expand[h, h * SSD_HEADDIM:(h + 1) * SSD_HEADDIM] = 1.0
    return dict(cos2=cos2, sin2=sin2, dmat=dmat, zeta=zeta_t, xi=xi_t, rdec=rdec_t, bias=bias_t,
                expand=jnp.asarray(expand, BF16))


def _regroup_w_in(w_in):
    offs = np.concatenate([[0], np.cumsum(IN_SPLIT_SIZES)])
    seg = [w_in[:, :, offs[i]:offs[i + 1]] for i in range(len(IN_SPLIT_SIZES))]
    z, xbc, dt, rq, rk, rv, rg, aq, ak, av, gate = seg
    dt = jnp.pad(dt, ((0, 0), (0, 0), (0, DT_W - SSD_HEADS)))
    aq = aq * (ATT_HEAD_DIM ** -0.5)
    return jnp.concatenate([xbc, rq, rk, rv, aq, ak, av, z, rg, gate, dt], axis=-1).astype(BF16)


def kernel(x, ffn1_pre_g, ffn1_post_g, w_ffn1_in, w_ffn1_out, mix_pre_g, mix_post_g, w_in, conv_w, conv_b, dt_bias, a_log, d_skip, ssd_norm_g, ret_gn_g, attn_sinks, rel_bias, b_gate, w_branch, w_out, ffn2_pre_g, ffn2_post_g, w_ffn2_in, w_ffn2_out):
    bsz, seq, _ = x.shape
    assert seq % CHUNK == 0
    t = bsz * seq
    nc = seq // CHUNK
    tm_ffn = min(512, t)
    tm_proj = min(256, seq)

    tabs = _tables(seq, rel_bias, tm_proj)
    row3 = lambda a: a.astype(F32)[:, None, :]
    pad16 = lambda a: jnp.pad(a.astype(F32), ((0, 0), (0, 128 - SSD_HEADS)))[:, None, :]
    lay = dict(
        conv_w=conv_w.astype(F32), conv_b=row3(conv_b),
        dt_bias=pad16(dt_bias), a_row=pad16(-jnp.exp(a_log.astype(F32))),
        d_skip=row3(jnp.repeat(d_skip, SSD_HEADDIM, axis=-1)),
        ssd_g=row3(ssd_norm_g), ret_g=row3(ret_gn_g))
    wi1, wo1 = w_ffn1_in.astype(BF16), w_ffn1_out.astype(BF16)
    wi2, wo2 = w_ffn2_in.astype(BF16), w_ffn2_out.astype(BF16)
    w_in_r = _regroup_w_in(w_in)
    wb, wo = w_branch.astype(BF16), w_out.astype(BF16)
    g1a, g1b = row3(ffn1_pre_g), row3(ffn1_post_g)
    g2a, g2b = row3(ffn2_pre_g), row3(ffn2_post_g)
    gma, gmb = row3(mix_pre_g), row3(mix_post_g)
    bg = row3(b_gate)
    sinks = attn_sinks.astype(F32)

    xf = x.reshape(t, D_MODEL)
    for l in range(DEPTH):
        xf = _ffn(xf, g1a, g1b, wi1, wo1, l, tm_ffn)
        pm, pz, dt = _inproj(xf, gma, w_in_r, lay, tabs, l, tm_proj, seq)
        ys = _mixer(pm, dt, sinks[l], tabs, lay, l, bsz, nc)
        xf = _merge(xf, ys, pz, bg, gmb, lay, wb, wo, l, tm_ffn)
        xf = _ffn(xf, g2a, g2b, wi2, wo2, l, tm_ffn)
    return xf.reshape(bsz, seq, D_MODEL)
```
